```python
import math
import jax, jax.numpy as jnp
from jax import lax
import numpy as np

D_MODEL = 2048
BATCH = 2
SEQ = 16384
DEPTH = 4
DEC_BATCH = 8
DEC_SEQ = 32
PAST_LEN = 4096

CHUNK = 64
Q_BLOCK = 128
Q_GROUP = 2048
HEAD_DIM = 128
N_HEADS = 4
DIFF_HALF = HEAD_DIM // 2
W_MIX = N_HEADS * HEAD_DIM
N_BRANCH = 3
IN_SIZES = (W_MIX,) * 9 + (N_HEADS, N_BRANCH * D_MODEL)
W_IN = sum(IN_SIZES)
D_FF = 2816
N_EXPERTS = 8
TOP_K = 2
D_FF_EXPERT = 352
N_BUCKETS = 32
MAX_DISTANCE = 128
EPS = 1e-6
N_DENSE = (DEPTH + 1) // 2
N_MOE = DEPTH // 2

kernel_name = "hybrid_sb_diff_fox_streaming_encoder_step"


def _rmsnorm(x, g):
    xf = x.astype(jnp.float32)
    y = xf * lax.rsqrt(jnp.mean(xf * xf, axis=-1, keepdims=True) + EPS)
    return (y * g.astype(jnp.float32)).astype(x.dtype)


def _t5_bucket(rel):
    nb = N_BUCKETS // 2
    max_exact = nb // 2
    base = jnp.where(rel > 0, nb, 0)
    n = jnp.abs(rel)
    nf = jnp.maximum(n, 1).astype(jnp.float32)
    large = max_exact + (jnp.log(nf / max_exact) / math.log(MAX_DISTANCE / max_exact)
                         * (nb - max_exact)).astype(jnp.int32)
    large = jnp.minimum(large, nb - 1)
    return base + jnp.where(n < max_exact, n, large)


def _sweep(block_fn, q_args, qpos, n_keys):
    tq = qpos.shape[0]
    if tq <= Q_BLOCK or tq % Q_BLOCK:
        return block_fn(n_keys, *q_args, qpos)
    offset = n_keys - tq
    outs = []
    for start in range(0, tq, Q_GROUP):
        size = min(Q_GROUP, tq - start)
        kend = offset + start + size
        nb = size // Q_BLOCK
        xs = tuple(jnp.moveaxis(a[:, start:start + size].reshape(a.shape[0], nb, Q_BLOCK, *a.shape[2:]), 1, 0)
                   for a in q_args) + (qpos[start:start + size].reshape(nb, Q_BLOCK),)
        out = lax.map(lambda t, kend=kend: block_fn(kend, *t), xs)
        out = jnp.moveaxis(out, 0, 1)
        outs.append(out.reshape(out.shape[0], size, *out.shape[3:]))
    return outs[0] if len(outs) == 1 else jnp.concatenate(outs, axis=1)


def _stick_breaking_block(q, k, v, qpos, kpos):
    z = jnp.einsum("bqhd,bkhd->bhqk", q, k).astype(jnp.float32) * (HEAD_DIM ** -0.5)
    mask = kpos[None, :] < qpos[:, None]
    ls = jax.nn.log_sigmoid(z)
    lm = jnp.where(mask, ls - z, 0.0)
    n_k = z.shape[-1]
    n_kb = -(-n_k // Q_BLOCK)
    lm4 = jnp.pad(lm, ((0, 0), (0, 0), (0, 0), (0, n_kb * Q_BLOCK - n_k))).reshape(
        *lm.shape[:3], n_kb, Q_BLOCK)
    i_in = jnp.arange(Q_BLOCK)
    u_in = (i_in[:, None] > i_in[None, :]).astype(jnp.float32)
    i_bk = jnp.arange(n_kb)
    u_bk = (i_bk[:, None] > i_bk[None, :]).astype(jnp.float32)
    within = jnp.einsum("bhqnj,js->bhqns", lm4, u_in)
    later = jnp.einsum("bhqm,mn->bhqn", jnp.sum(lm4, axis=-1), u_bk)
    tail = (within + later[..., None]).reshape(*lm.shape[:3], n_kb * Q_BLOCK)[..., :n_k]
    w = jnp.where(mask, jnp.exp(ls + tail), 0.0)
    return jnp.einsum("bhqk,bkhd->bqhd", w.astype(v.dtype), v)


def _diff_block(q, k, v, lam, rel_bias, qpos, kpos):
    scale = DIFF_HALF ** -0.5
    s1 = jnp.einsum("bqhd,bkhd->bhqk", q[..., :DIFF_HALF], k[..., :DIFF_HALF]).astype(jnp.float32)
    s2 = jnp.einsum("bqhd,bkhd->bhqk", q[..., DIFF_HALF:], k[..., DIFF_HALF:]).astype(jnp.float32)
    bias = jnp.take(rel_bias, _t5_bucket(kpos[None, :] - qpos[:, None]), axis=0)
    bias = jnp.transpose(bias, (2, 0, 1))[None].astype(jnp.float32)
    mask = (kpos // CHUNK)[None, :] <= (qpos // CHUNK)[:, None]

    def smax(s):
        return jax.nn.softmax(jnp.where(mask, s * scale + bias, -jnp.inf), axis=-1)

    attn = smax(s1) - lam * smax(s2)
    return jnp.einsum("bhqk,bkhd->bqhd", attn.astype(v.dtype), v)


def _fox_block(q, k, v, dq, dk, qpos, kpos):
    s = jnp.einsum("bqhd,bkhd->bhqk", q, k).astype(jnp.float32) * (HEAD_DIM ** -0.5)
    s = s + jnp.transpose(dq, (0, 2, 1))[..., :, None] - jnp.transpose(dk, (0, 2, 1))[..., None, :]
    mask = kpos[None, :] <= qpos[:, None]
    p = jax.nn.softmax(jnp.where(mask, s, -jnp.inf), axis=-1)
    return jnp.einsum("bhqk,bkhd->bqhd", p.astype(v.dtype), v)


def _split_cols(proj):
    outs, o = [], 0
    for s in IN_SIZES:
        outs.append(proj[..., o:o + s])
        o += s
    return outs


def _mixer(h, past, qpos, kpos, l, P):
    B, T, _ = h.shape
    proj = h @ P["w_in"][l]
    qa, ka, va, qb, kb, vb, qc, kc, vc, f_pre, gate_pre = _split_cols(proj)
    hd = lambda a: a.reshape(B, T, N_HEADS, HEAD_DIM)
    qa, ka, va, qb, kb, vb, qc, kc, vc = (hd(a) for a in (qa, ka, va, qb, kb, vb, qc, kc, vc))
    logf = jax.nn.log_sigmoid(f_pre.astype(jnp.float32) + P["b_forget"][l].astype(jnp.float32))
    rows = (ka, va, kb, vb, kc, vc, logf)
    if past is None:
        ka_all, va_all, kb_all, vb_all, kc_all, vc_all, lf_all = rows
    else:
        ka_all, va_all, kb_all, vb_all, kc_all, vc_all, lf_all = (
            jnp.concatenate([p.astype(n.dtype), n], axis=1) for p, n in zip(past, rows))
    n_keys = ka_all.shape[1]
    dcum = jnp.cumsum(lf_all.astype(jnp.float32), axis=1)
    dq = dcum[:, -T:]

    oa = _sweep(lambda ke, q, qp: _stick_breaking_block(q, ka_all[:, :ke], va_all[:, :ke], qp, kpos[:ke]),
                (qa,), qpos, n_keys)
    lam_init = 0.8 - 0.6 * math.exp(-0.3 * l)
    lam = (jnp.exp(jnp.sum(P["lambda_q1"][l].astype(jnp.float32) * P["lambda_k1"][l].astype(jnp.float32)))
           - jnp.exp(jnp.sum(P["lambda_q2"][l].astype(jnp.float32) * P["lambda_k2"][l].astype(jnp.float32)))
           + lam_init)
    ob = _sweep(lambda ke, q, qp: _diff_block(q, kb_all[:, :ke], vb_all[:, :ke], lam, P["rel_bias"], qp, kpos[:ke]),
                (qb,), qpos, n_keys)
    ob = _rmsnorm(ob, P["diff_norm_g"][l]) * (1.0 - lam_init)
    oc = _sweep(lambda ke, q, d, qp: _fox_block(q, kc_all[:, :ke], vc_all[:, :ke], d, dcum[:, :ke], qp, kpos[:ke]),
                (qc, dq), qpos, n_keys)

    gates = jax.nn.sigmoid(gate_pre.astype(jnp.float32)).astype(h.dtype).reshape(B, T, N_BRANCH, D_MODEL)
    merged = None
    for i, o in enumerate((oa, ob, oc)):
        term = gates[:, :, i] * (o.reshape(B, T, W_MIX).astype(h.dtype) @ P["w_branch"][l, i])
        merged = term if merged is None else merged + term
    return merged @ P["w_out"][l], rows


def _swiglu(h, wg, wu, wd):
    return (jax.nn.silu(h @ wg) * (h @ wu)) @ wd


def _moe(h, w_router, b_router, wg, wu, wd):
    logits = (h @ w_router).astype(jnp.float32) + b_router.astype(jnp.float32)
    top_v, top_i = lax.top_k(logits, TOP_K)
    probs = jax.nn.softmax(top_v, axis=-1)
    combine = jnp.sum(jax.nn.one_hot(top_i, N_EXPERTS, dtype=jnp.float32) * probs[..., None], axis=-2)
    out = None
    for e in range(N_EXPERTS):
        term = combine[..., e:e + 1].astype(h.dtype) * _swiglu(h, wg[e], wu[e], wd[e])
        out = term if out is None else out + term
    return out


def _trunk(x, c, past, qpos, kpos, P):
    new = []
    for l in range(DEPTH):
        mod = (jax.nn.silu(c) @ P["w_ada"][l] + P["b_ada"][l])[:, None, :]
        sh1, sc1, g1, sh2, sc2, g2 = jnp.split(mod, 6, axis=-1)
        h = _rmsnorm(x, P["norm1_g"][l]) * (1 + sc1) + sh1
        layer_past = None if past is None else tuple(p[l] for p in past)
        y, rows = _mixer(h, layer_past, qpos, kpos, l, P)
        x = x + g1 * y
        h = _rmsnorm(x, P["norm2_g"][l]) * (1 + sc2) + sh2
        m = l // 2
        if l % 2 == 0:
            f = _swiglu(h, P["w_ffn_gate"][m], P["w_ffn_up"][m], P["w_ffn_down"][m])
        else:
            f = _moe(h, P["w_router"][m], P["b_router"][m], P["w_exp_gate"][m],
                     P["w_exp_up"][m], P["w_exp_down"][m])
        x = x + g2 * f
        new.append(rows)
    y = _rmsnorm(x, P["final_norm_g"])
    stacked = tuple(jnp.stack([r[i] for r in new]) for i in range(7))
    return y, stacked


def setup_inputs(seed: int = 0) -> dict:
    key = jax.random.key(seed)
    ks = iter(jax.random.split(key, 48))
    f32 = jnp.float32

    def nrm(shape, s):
        return jax.random.normal(next(ks), shape, f32) * s

    cshape = (DEPTH, DEC_BATCH, PAST_LEN, N_HEADS, HEAD_DIM)
    return {
        "x_prompt": nrm((BATCH, SEQ, D_MODEL), 1.0),
        "x_sample": nrm((DEC_BATCH, DEC_SEQ, D_MODEL), 1.0),
        "c_prompt": nrm((BATCH, D_MODEL), 1.0),
        "c_sample": nrm((DEC_BATCH, D_MODEL), 1.0),
        "cache_k_a": nrm(cshape, 1.0),
        "cache_v_a": nrm(cshape, 1.0),
        "cache_k_b": nrm(cshape, 1.0),
        "cache_v_b": nrm(cshape, 1.0),
        "cache_k_c": nrm(cshape, 1.0),
        "cache_v_c": nrm(cshape, 1.0),
        "cache_lf_c": jax.nn.log_sigmoid(2.0 + nrm((DEPTH, DEC_BATCH, PAST_LEN, N_HEADS), 0.5)),
        "w_ada": nrm((DEPTH, D_MODEL, 6 * D_MODEL), 0.5 * D_MODEL ** -0.5),
        "b_ada": nrm((DEPTH, 6 * D_MODEL), 0.02),
        "norm1_g": 1.0 + nrm((DEPTH, D_MODEL), 0.02),
        "norm2_g": 1.0 + nrm((DEPTH, D_MODEL), 0.02),
        "w_in": nrm((DEPTH, D_MODEL, W_IN), D_MODEL ** -0.5),
        "b_forget": 2.0 + nrm((DEPTH, N_HEADS), 0.1),
        "lambda_q1": nrm((DEPTH, DIFF_HALF), 0.1),
        "lambda_k1": nrm((DEPTH, DIFF_HALF), 0.1),
        "lambda_q2": nrm((DEPTH, DIFF_HALF), 0.1),
        "lambda_k2": nrm((DEPTH, DIFF_HALF), 0.1),
        "diff_norm_g": 1.0 + nrm((DEPTH, HEAD_DIM), 0.02),
        "rel_bias": nrm((N_BUCKETS, N_HEADS), 0.5),
        "w_branch": nrm((DEPTH, N_BRANCH, W_MIX, D_MODEL), W_MIX ** -0.5),
        "w_out": nrm((DEPTH, D_MODEL, D_MODEL), D_MODEL ** -0.5),
        "w_ffn_gate": nrm((N_DENSE, D_MODEL, D_FF), D_MODEL ** -0.5),
        "w_ffn_up": nrm((N_DENSE, D_MODEL, D_FF), D_MODEL ** -0.5),
        "w_ffn_down": nrm((N_DENSE, D_FF, D_MODEL), D_FF ** -0.5),
        "w_router": nrm((N_MOE, D_MODEL, N_EXPERTS), D_MODEL ** -0.5),
        "b_router": nrm((N_MOE, N_EXPERTS), 0.01),
        "w_exp_gate": nrm((N_MOE, N_EXPERTS, D_MODEL, D_FF_EXPERT), D_MODEL ** -0.5),
        "w_exp_up": nrm((N_MOE, N_EXPERTS, D_MODEL, D_FF_EXPERT), D_MODEL ** -0.5),
        "w_exp_down": nrm((N_MOE, N_EXPERTS, D_FF_EXPERT, D_MODEL), D_FF_EXPERT ** -0.5),
        "final_norm_g": 1.0 + nrm((D_MODEL,), 0.02),
    }


def reference(x_prompt, x_sample, c_prompt, c_sample, cache_k_a, cache_v_a, cache_k_b, cache_v_b,
              cache_k_c, cache_v_c, cache_lf_c, w_ada, b_ada, norm1_g, norm2_g, w_in, b_forget,
              lambda_q1, lambda_k1, lambda_q2, lambda_k2, diff_norm_g, rel_bias, w_branch, w_out,
              w_ffn_gate, w_ffn_up, w_ffn_down, w_router, b_router, w_exp_gate, w_exp_up, w_exp_down,
              final_norm_g):
    P = dict(w_ada=w_ada, b_ada=b_ada, norm1_g=norm1_g, norm2_g=norm2_g, w_in=w_in, b_forget=b_forget,
             lambda_q1=lambda_q1, lambda_k1=lambda_k1, lambda_q2=lambda_q2, lambda_k2=lambda_k2,
             diff_norm_g=diff_norm_g, rel_bias=rel_bias, w_branch=w_branch, w_out=w_out,
             w_ffn_gate=w_ffn_gate, w_ffn_up=w_ffn_up, w_ffn_down=w_ffn_down, w_router=w_router,
             b_router=b_router, w_exp_gate=w_exp_gate, w_exp_up=w_exp_up, w_exp_down=w_exp_down,
             final_norm_g=final_norm_g)
    t_p = x_prompt.shape[1]
    pos_p = jnp.arange(t_p, dtype=jnp.int32)
    y_prompt, st_p = _trunk(x_prompt, c_prompt, None, pos_p, pos_p, P)
    t_s = x_sample.shape[1]
    past_len = cache_k_a.shape[2]
    qpos_s = past_len + jnp.arange(t_s, dtype=jnp.int32)
    kpos_s = jnp.arange(past_len + t_s, dtype=jnp.int32)
    past = (cache_k_a, cache_v_a, cache_k_b, cache_v_b, cache_k_c, cache_v_c, cache_lf_c)
    y_sample, st_s = _trunk(x_sample, c_sample, past, qpos_s, kpos_s, P)
    new_k_a_p, new_v_a_p, new_k_b_p, new_v_b_p, new_k_c_p, new_v_c_p, new_lf_c_p = st_p
    new_k_a_s, new_v_a_s, new_k_b_s, new_v_b_s, new_k_c_s, new_v_c_s, new_lf_c_s = st_s
    return (y_prompt, y_sample,
            new_k_a_p, new_v_a_p, new_k_b_p, new_v_b_p, new_k_c_p, new_v_c_p, new_lf_c_p,
            new_k_a_s, new_v_a_s, new_k_b_s, new_v_b_s, new_k_c_s, new_v_c_s, new_lf_c_s)
```

```python
import functools
import math

import jax
import jax.numpy as jnp
from jax import lax
from jax.experimental import pallas as pl
from jax.experimental.pallas import tpu as pltpu

F32, BF16 = jnp.float32, jnp.bfloat16

CHUNK = 64
HEAD_DIM = 128
N_HEADS = 4
DIFF_HALF = HEAD_DIM // 2
W_MIX = N_HEADS * HEAD_DIM
N_BRANCH = 3
N_BUCKETS = 32
MAX_DISTANCE = 128
EPS = 1e-6
NEG = -1e30

LANES = 128
VMEM_LIMIT = 56 * 1024 * 1024


def _cparams(*sem):
    return pltpu.CompilerParams(dimension_semantics=sem, vmem_limit_bytes=VMEM_LIMIT)


def _dot(a, b):
    return jnp.dot(a, b, preferred_element_type=F32)


def _dot_nt(a, b):
    return lax.dot_general(a, b, (((1,), (1,)), ((), ())), preferred_element_type=F32)


def _norm_mod(x, g, sc, sh):
    var = jnp.mean(x * x, axis=-1, keepdims=True)
    y = x * lax.rsqrt(var + EPS) * g
    return y * (1.0 + sc) + sh


def _mod_rows(ref, sl):
    if ref.shape[2] == 1:
        return ref[0, 0]
    return ref[0, 0, sl, :]


def _log1pexp_neg_abs(z):
    return jnp.log1p(jnp.exp(-jnp.abs(z)))


def _ada_kernel(c_ref, w_ref, b_ref, o_ref):
    c = c_ref[...]
    a = (c * jax.nn.sigmoid(c)).astype(BF16)
    o_ref[0] = _dot(a, w_ref[0].astype(BF16)) + b_ref[0]


def _ada(c_all, w_ada, b_ada):
    depth, d, n = w_ada.shape
    rows = c_all.shape[0]
    tn = 1024 if n % 1024 == 0 else n
    return pl.pallas_call(
        _ada_kernel,
        grid=(depth, n // tn),
        in_specs=[
            pl.BlockSpec((rows, d), lambda l, j: (0, 0)),
            pl.BlockSpec((1, d, tn), lambda l, j: (l, 0, j)),
            pl.BlockSpec((1, 1, tn), lambda l, j: (l, 0, j)),
        ],
        out_specs=pl.BlockSpec((1, rows, tn), lambda l, j: (l, 0, j)),
        out_shape=jax.ShapeDtypeStruct((depth, rows, n), F32),
        compiler_params=_cparams("parallel", "parallel"),
        name="ada",
    )(c_all, w_ada, b_ada.reshape(depth, 1, n))


def _inproj_kernel(x_ref, g_ref, sc_ref, sh_ref, wf_ref, bf_ref, w_ref,
                   q_ref, kv32_ref, kv16_ref, gate_ref, lf_ref, h_ref, *, rows):
    j = pl.program_id(1)
    tm = x_ref.shape[0]

    @pl.when(j == 0)
    def _prologue():
        def chunk(r, carry):
            sl = pl.ds(pl.multiple_of(r * rows, rows), rows)
            h = _norm_mod(x_ref[sl, :], g_ref[...], _mod_rows(sc_ref, sl), _mod_rows(sh_ref, sl)).astype(BF16)
            h_ref[sl, :] = h
            f = _dot(h, wf_ref[...]) + bf_ref[...]
            lf_ref[sl, :] = jnp.minimum(f, 0.0) - _log1pexp_neg_abs(f)
            return carry
        lax.fori_loop(0, tm // rows, chunk, 0)

    acc = _dot(h_ref[...], w_ref[0])

    @pl.when(j < 3)
    def _q():
        scale = jnp.where(j == 1, DIFF_HALF ** -0.5, HEAD_DIM ** -0.5)
        q_ref[0] = (acc * scale).astype(BF16)

    @pl.when(jnp.logical_and(j >= 3, j < 9))
    def _kv():
        kv32_ref[0] = acc
        kv16_ref[0] = acc.astype(BF16)

    @pl.when(j >= 9)
    def _gate():
        gate_ref[...] = jax.nn.sigmoid(acc).astype(BF16)


def _mod_spec(mod, comp, tm, tn, tiles_per_batch, col_of=lambda j: 0, with_j=True):
    r = mod.shape[2]
    if r == 1:
        if with_j:
            return pl.BlockSpec((1, 1, 1, tn), lambda i, j: (i // tiles_per_batch, comp, 0, col_of(j)))
        return pl.BlockSpec((1, 1, 1, tn), lambda i: (i // tiles_per_batch, comp, 0, 0))
    if with_j:
        return pl.BlockSpec((1, 1, tm, tn), lambda i, j: (0, comp, i, col_of(j)))
    return pl.BlockSpec((1, 1, tm, tn), lambda i: (0, comp, i, 0))


def _inproj(x, mod, norm_g, w_main, w_f, b_f, layer, tm, tiles_per_batch):
    n, d = x.shape
    n_gate = w_main.shape[2] // W_MIX - 9
    grid = (n // tm, 9 + n_gate)
    rows = min(tm, 256)
    outs = pl.pallas_call(
        functools.partial(_inproj_kernel, rows=rows),
        grid=grid,
        in_specs=[
            pl.BlockSpec((tm, d), lambda i, j: (i, 0)),
            pl.BlockSpec((1, d), lambda i, j: (0, 0)),
            _mod_spec(mod, 1, tm, d, tiles_per_batch),
            _mod_spec(mod, 0, tm, d, tiles_per_batch),
            pl.BlockSpec((d, LANES), lambda i, j: (0, 0)),
            pl.BlockSpec((1, LANES), lambda i, j: (0, 0)),
            pl.BlockSpec((1, d, W_MIX), lambda i, j: (layer, 0, j)),
        ],
        out_specs=[
            pl.BlockSpec((1, tm, W_MIX), lambda i, j: (jnp.minimum(j, 2), i, 0)),
            pl.BlockSpec((1, tm, W_MIX), lambda i, j: (jnp.clip(j - 3, 0, 5), i, 0)),
            pl.BlockSpec((1, tm, W_MIX), lambda i, j: (jnp.clip(j - 3, 0, 5), i, 0)),
            pl.BlockSpec((tm, W_MIX), lambda i, j: (i, jnp.clip(j - 9, 0, n_gate - 1))),
            pl.BlockSpec((tm, LANES), lambda i, j: (i, 0)),
        ],
        out_shape=[
            jax.ShapeDtypeStruct((3, n, W_MIX), BF16),
            jax.ShapeDtypeStruct((6, n, W_MIX), F32),
            jax.ShapeDtypeStruct((6, n, W_MIX), BF16),
            jax.ShapeDtypeStruct((n, n_gate * W_MIX), BF16),
            jax.ShapeDtypeStruct((n, LANES), F32),
        ],
        scratch_shapes=[pltpu.VMEM((tm, d), BF16)],
        compiler_params=_cparams("parallel", "arbitrary"),
        name="inproj",
    )(x, norm_g, mod, mod, w_f, b_f, w_main)
    return outs


def _split3(a):
    a1 = a.astype(BF16)
    r1 = a - a1.astype(F32)
    a2 = r1.astype(BF16)
    a3 = (r1 - a2.astype(F32)).astype(BF16)
    return a1, a2, a3


def _cumsum_kernel(x_ref, o_ref):
    x = x_ref[0]
    r = x.shape[0]
    ur = lax.broadcasted_iota(jnp.int32, (LANES, LANES), 0)
    uc = lax.broadcasted_iota(jnp.int32, (LANES, LANES), 1)
    upper = (ur <= uc).astype(BF16)
    within = sum(_dot(p, upper) for p in _split3(x))
    tot = jnp.broadcast_to(within[:, LANES - 1:LANES], (r, LANES))
    lr = lax.broadcasted_iota(jnp.int32, (r, r), 0)
    lc = lax.broadcasted_iota(jnp.int32, (r, r), 1)
    lower = (lc < lr).astype(BF16)
    carry = sum(_dot(lower, p) for p in _split3(tot))
    o_ref[0] = within + carry


def _cumsum_rows(x):
    g, r, _ = x.shape
    return pl.pallas_call(
        _cumsum_kernel,
        grid=(g,),
        in_specs=[pl.BlockSpec((1, r, LANES), lambda i: (i, 0, 0))],
        out_specs=pl.BlockSpec((1, r, LANES), lambda i: (i, 0, 0)),
        out_shape=jax.ShapeDtypeStruct((g, r, LANES), F32),
        compiler_params=_cparams("parallel"),
        name="cumsum",
    )(x)


def _rel_iota(tq, tk):
    return lax.broadcasted_iota(jnp.int32, (tq, tk), 1) - lax.broadcasted_iota(jnp.int32, (tq, tk), 0)


def _strict_lower(tk):
    r = lax.broadcasted_iota(jnp.int32, (tk, tk), 0)
    c = lax.broadcasted_iota(jnp.int32, (tk, tk), 1)
    return (r > c).astype(BF16)


def _sb_tile(q, k, v, tri, r_sc, acc_sc, thresh):
    z = _dot_nt(q, k)
    lp = _log1pexp_neg_abs(z)
    sp = jnp.maximum(z, 0.0) + lp
    ls = jnp.minimum(z, 0.0) - lp
    if thresh is not None:
        vis = _rel_iota(*z.shape) < thresh
        sp = jnp.where(vis, sp, 0.0)
    later = _dot(sp.astype(BF16), tri)
    w = jnp.exp(ls - later - r_sc[...])
    if thresh is not None:
        w = jnp.where(vis, w, 0.0)
    acc_sc[...] += _dot(w.astype(BF16), v)
    r_sc[...] += jnp.sum(sp, axis=-1, keepdims=True)


def _softmax_tile(s, v, m_sc, l_sc, acc_sc, shift=None):
    m_prev = m_sc[...]
    row_max = jnp.max(s, axis=-1, keepdims=True)
    if shift is not None:
        row_max = row_max + shift
    m_new = jnp.maximum(m_prev, row_max)
    alpha = jnp.exp(m_prev - m_new)
    p = jnp.exp(s - (m_new if shift is None else m_new - shift))
    l_sc[...] = alpha * l_sc[...] + jnp.sum(p, axis=-1, keepdims=True)
    acc_sc[...] = alpha * acc_sc[...] + _dot(p.astype(BF16), v)
    m_sc[...] = m_new


def _init_softmax(m_sc, l_sc, acc_sc):
    m_sc[...] = jnp.full(m_sc.shape, NEG, F32)
    l_sc[...] = jnp.zeros(l_sc.shape, F32)
    acc_sc[...] = jnp.zeros(acc_sc.shape, F32)


def _split_halves(q):
    lane = lax.broadcasted_iota(jnp.int32, q.shape, 1)
    zero = jnp.zeros_like(q)
    return jnp.where(lane < DIFF_HALF, q, zero), jnp.where(lane >= DIFF_HALF, q, zero)


def _diff_finish(lam_ref, gn_ref, o_ref, l1_sc, acc1_sc, l2_sc, acc2_sc, lam_init):
    lp = lam_ref[...]
    lam = (jnp.exp(jnp.sum(lp[0:1] * lp[1:2], axis=-1, keepdims=True))
           - jnp.exp(jnp.sum(lp[2:3] * lp[3:4], axis=-1, keepdims=True)) + lam_init)
    out = acc1_sc[...] / l1_sc[...] - lam * (acc2_sc[...] / l2_sc[...])
    var = jnp.mean(out * out, axis=-1, keepdims=True)
    o_ref[...] = (out * lax.rsqrt(var + EPS) * gn_ref[...] * (1.0 - lam_init)).astype(BF16)


def _sb_p_kernel(q_ref, k_ref, v_ref, o_ref, r_sc, acc_sc, *, tk):
    qi = pl.program_id(2)
    tq = q_ref.shape[0]
    q = q_ref[...]
    tri = _strict_lower(tk)
    r_sc[...] = jnp.zeros(r_sc.shape, F32)
    acc_sc[...] = jnp.zeros(acc_sc.shape, F32)
    q0 = qi * tq

    def tile(jt, masked):
        start = pl.multiple_of(jt * tk, tk)
        k = k_ref[pl.ds(start, tk), :]
        v = v_ref[pl.ds(start, tk), :]
        _sb_tile(q, k, v, tri, r_sc, acc_sc, (q0 - start) if masked else None)

    for u in reversed(range(tq // tk)):
        tile(q0 // tk + u, True)

    def body(it, carry):
        tile(q0 // tk - 1 - it, False)
        return carry
    lax.fori_loop(0, q0 // tk, body, 0)
    o_ref[...] = acc_sc[...].astype(BF16)


def _fox_p_kernel(d0_ref, q_ref, k_ref, v_ref, dk_ref, o_ref, m_sc, l_sc, acc_sc, *, tk):
    b, h, qi = pl.program_id(0), pl.program_id(1), pl.program_id(2)
    nq = pl.num_programs(2)
    tq = q_ref.shape[0]
    q = q_ref[...]
    d0 = d0_ref[(b * N_HEADS + h) * nq + qi]
    _init_softmax(m_sc, l_sc, acc_sc)
    q0 = qi * tq

    def tile(jt, masked):
        start = pl.multiple_of(jt * tk, tk)
        k = k_ref[pl.ds(start, tk), :]
        v = v_ref[pl.ds(start, tk), :]
        s = _dot_nt(q, k) + (d0 - dk_ref[0, jt])
        if masked:
            s = jnp.where(_rel_iota(tq, tk) <= q0 - start, s, NEG)
        _softmax_tile(s, v, m_sc, l_sc, acc_sc)

    def body(jt, carry):
        tile(jt, False)
        return carry
    lax.fori_loop(0, q0 // tk, body, 0)
    for u in range(tq // tk):
        tile(q0 // tk + u, True)
    o_ref[...] = (acc_sc[...] / l_sc[...]).astype(BF16)


def _diff_p_kernel(cfar_ref, q_ref, k_ref, v_ref, bsub_ref, bdiag_ref, lam_ref, gn_ref, o_ref,
                   m1_sc, l1_sc, acc1_sc, m2_sc, l2_sc, acc2_sc, *, lam_init):
    h, qi = pl.program_id(1), pl.program_id(2)
    tq = q_ref.shape[0]
    tk = tq
    q1, q2 = _split_halves(q_ref[...])
    cfar = cfar_ref[h]
    _init_softmax(m1_sc, l1_sc, acc1_sc)
    _init_softmax(m2_sc, l2_sc, acc2_sc)

    def tile(jt, bias):
        start = pl.multiple_of(jt * tk, tk)
        k = k_ref[pl.ds(start, tk), :]
        v = v_ref[pl.ds(start, tk), :]
        s1 = _dot_nt(q1, k)
        s2 = _dot_nt(q2, k)
        if bias is None:
            _softmax_tile(s1, v, m1_sc, l1_sc, acc1_sc, shift=cfar)
            _softmax_tile(s2, v, m2_sc, l2_sc, acc2_sc, shift=cfar)
        else:
            _softmax_tile(s1 + bias, v, m1_sc, l1_sc, acc1_sc)
            _softmax_tile(s2 + bias, v, m2_sc, l2_sc, acc2_sc)

    def body(jt, carry):
        tile(jt, None)
        return carry
    lax.fori_loop(0, jnp.maximum(qi - 1, 0), body, 0)

    @pl.when(qi >= 1)
    def _sub():
        tile(qi - 1, bsub_ref[0])

    tile(qi, bdiag_ref[0])
    _diff_finish(lam_ref, gn_ref, o_ref, l1_sc, acc1_sc, l2_sc, acc2_sc, lam_init)


def _attn_specs(which_q, which_k, which_v, t, tq):
    nq = t // tq
    q_spec = pl.BlockSpec((None, tq, HEAD_DIM), lambda b, h, i: (which_q, b * nq + i, h))
    k_spec = pl.BlockSpec((None, t, HEAD_DIM), lambda b, h, i: (which_k, b, h))
    v_spec = pl.BlockSpec((None, t, HEAD_DIM), lambda b, h, i: (which_v, b, h))
    o_spec = pl.BlockSpec((tq, HEAD_DIM), lambda b, h, i: (b * nq + i, h))
    return q_spec, k_spec, v_spec, o_spec


def _sb_prompt(q3, kv16, nb, t, tq, tk):
    n = nb * t
    q_spec, k_spec, v_spec, o_spec = _attn_specs(0, 0, 1, t, tq)
    return pl.pallas_call(
        functools.partial(_sb_p_kernel, tk=tk),
        grid=(nb, N_HEADS, t // tq),
        in_specs=[q_spec, k_spec, v_spec],
        out_specs=o_spec,
        out_shape=jax.ShapeDtypeStruct((n, W_MIX), BF16),
        scratch_shapes=[pltpu.VMEM((tq, 1), F32), pltpu.VMEM((tq, HEAD_DIM), F32)],
        compiler_params=_cparams("parallel", "parallel", "arbitrary"),
        name="sb_prompt",
    )(q3, kv16, kv16)


def _fox_prompt(q3, kv16, dcum, nb, t, tq, tk):
    n = nb * t
    nq = t // tq
    q_spec, k_spec, v_spec, o_spec = _attn_specs(2, 4, 5, t, tq)
    d0 = dcum[:, ::tq].reshape(-1)
    dk = dcum.reshape(nb * N_HEADS, t // tk, 1, tk)
    return pl.pallas_call(
        functools.partial(_fox_p_kernel, tk=tk),
        grid=(nb, N_HEADS, nq),
        in_specs=[
            pl.BlockSpec(memory_space=pltpu.SMEM),
            q_spec, k_spec, v_spec,
            pl.BlockSpec((1, t // tk, 1, tk), lambda b, h, i: (b * N_HEADS + h, 0, 0, 0)),
        ],
        out_specs=o_spec,
        out_shape=jax.ShapeDtypeStruct((n, W_MIX), BF16),
        scratch_shapes=[pltpu.VMEM((tq, 1), F32), pltpu.VMEM((tq, 1), F32), pltpu.VMEM((tq, HEAD_DIM), F32)],
        compiler_params=_cparams("parallel", "parallel", "arbitrary"),
        name="fox_prompt",
    )(d0, q3, kv16, kv16, dk)


def _diff_prompt(q3, kv16, cfar, bsub, bdiag, lam_rows, gn, lam_init, nb, t, tq):
    n = nb * t
    q_spec, k_spec, v_spec, o_spec = _attn_specs(1, 2, 3, t, tq)
    stat = [pltpu.VMEM((tq, 1), F32), pltpu.VMEM((tq, 1), F32), pltpu.VMEM((tq, HEAD_DIM), F32)]
    return pl.pallas_call(
        functools.partial(_diff_p_kernel, lam_init=lam_init),
        grid=(nb, N_HEADS, t // tq),
        in_specs=[
            pl.BlockSpec(memory_space=pltpu.SMEM),
            q_spec, k_spec, v_spec,
            pl.BlockSpec((1, tq, tq), lambda b, h, i: (h, 0, 0)),
            pl.BlockSpec((1, tq, tq), lambda b, h, i: (h, 0, 0)),
            pl.BlockSpec((8, LANES), lambda b, h, i: (0, 0)),
            pl.BlockSpec((1, HEAD_DIM), lambda b, h, i: (0, 0)),
        ],
        out_specs=o_spec,
        out_shape=jax.ShapeDtypeStruct((n, W_MIX), BF16),
        scratch_shapes=stat + stat,
        compiler_params=_cparams("parallel", "parallel", "arbitrary"),
        name="diff_prompt",
    )(cfar, q3, kv16, kv16, bsub, bdiag, lam_rows, gn)


def _pad_rows(a, rows):
    return jnp.concatenate([a, jnp.zeros((rows - a.shape[0], a.shape[1]), a.dtype)], axis=0)


def _sb_s_kernel(q_ref, kp_ref, vp_ref, kn_ref, vn_ref, o_ref, r_sc, acc_sc, *, tk):
    q = q_ref[...]
    p = kp_ref.shape[0]
    r_sc[...] = jnp.zeros(r_sc.shape, F32)
    acc_sc[...] = jnp.zeros(acc_sc.shape, F32)
    _sb_tile(q, _pad_rows(kn_ref[...], LANES), _pad_rows(vn_ref[...], LANES), _strict_lower(LANES), r_sc, acc_sc, 0)
    tri = _strict_lower(tk)

    def body(it, carry):
        start = pl.multiple_of((p // tk - 1 - it) * tk, tk)
        k = kp_ref[pl.ds(start, tk), :].astype(BF16)
        v = vp_ref[pl.ds(start, tk), :].astype(BF16)
        _sb_tile(q, k, v, tri, r_sc, acc_sc, None)
        return carry
    lax.fori_loop(0, p // tk, body, 0)
    o_ref[...] = acc_sc[...].astype(BF16)


def _fox_s_kernel(d0_ref, q_ref, kp_ref, vp_ref, kn_ref, vn_ref, d_ref, o_ref):
    b, h = pl.program_id(0), pl.program_id(1)
    q = q_ref[...]
    tq = q.shape[0]
    p = kp_ref.shape[0]
    d0 = d0_ref[b * N_HEADS + h]
    d = d_ref[0]
    s_p = _dot_nt(q, kp_ref[...].astype(BF16)) + (d0 - d[:, :p])
    s_n = _dot_nt(q, _pad_rows(kn_ref[...], LANES)) + (d0 - d[:, p:p + LANES])
    s_n = jnp.where(_rel_iota(tq, LANES) <= 0, s_n, NEG)
    m = jnp.maximum(jnp.max(s_p, axis=-1, keepdims=True), jnp.max(s_n, axis=-1, keepdims=True))
    p_p = jnp.exp(s_p - m)
    p_n = jnp.exp(s_n - m)
    l = jnp.sum(p_p, axis=-1, keepdims=True) + jnp.sum(p_n, axis=-1, keepdims=True)
    acc = _dot(p_p.astype(BF16), vp_ref[...].astype(BF16)) + _dot(p_n.astype(BF16), _pad_rows(vn_ref[...], LANES))
    o_ref[...] = (acc / l).astype(BF16)


def _diff_s_kernel(q_ref, kp_ref, vp_ref, kn_ref, vn_ref, bp_ref, bn_ref, lam_ref, gn_ref, o_ref,
                   l1_sc, acc1_sc, l2_sc, acc2_sc, *, lam_init):
    q1, q2 = _split_halves(q_ref[...])
    kp = kp_ref[...].astype(BF16)
    vp = vp_ref[...].astype(BF16)
    kn = _pad_rows(kn_ref[...], LANES)
    vn = _pad_rows(vn_ref[...], LANES)

    def one(qh, l_sc, acc_sc):
        s_p = _dot_nt(qh, kp) + bp_ref[0]
        s_n = _dot_nt(qh, kn) + bn_ref[0]
        m = jnp.maximum(jnp.max(s_p, axis=-1, keepdims=True), jnp.max(s_n, axis=-1, keepdims=True))
        p_p = jnp.exp(s_p - m)
        p_n = jnp.exp(s_n - m)
        l_sc[...] = jnp.sum(p_p, axis=-1, keepdims=True) + jnp.sum(p_n, axis=-1, keepdims=True)
        acc_sc[...] = _dot(p_p.astype(BF16), vp) + _dot(p_n.astype(BF16), vn)

    one(q1, l1_sc, acc1_sc)
    one(q2, l2_sc, acc2_sc)
    _diff_finish(lam_ref, gn_ref, o_ref, l1_sc, acc1_sc, l2_sc, acc2_sc, lam_init)


def _sample_specs(which_q, which_k, which_v, layer, ts, p):
    q_spec = pl.BlockSpec((None, ts, HEAD_DIM), lambda b, h: (which_q, b, h))
    kp_spec = pl.BlockSpec((None, None, p, HEAD_DIM), lambda b, h: (layer, b, 0, h))
    kn_spec = pl.BlockSpec((None, ts, HEAD_DIM), lambda b, h: (which_k, b, h))
    vn_spec = pl.BlockSpec((None, ts, HEAD_DIM), lambda b, h: (which_v, b, h))
    o_spec = pl.BlockSpec((ts, HEAD_DIM), lambda b, h: (b, h))
    return q_spec, kp_spec, kn_spec, vn_spec, o_spec


def _sb_sample(q3, kv16, ck, cv, layer, nb, ts, tk):
    p = ck.shape[2]
    q_spec, kp_spec, kn_spec, vn_spec, o_spec = _sample_specs(0, 0, 1, layer, ts, p)
    return pl.pallas_call(
        functools.partial(_sb_s_kernel, tk=tk),
        grid=(nb, N_HEADS),
        in_specs=[q_spec, kp_spec, kp_spec, kn_spec, vn_spec],
        out_specs=o_spec,
        out_shape=jax.ShapeDtypeStruct((nb * ts, W_MIX), BF16),
        scratch_shapes=[pltpu.VMEM((ts, 1), F32), pltpu.VMEM((ts, HEAD_DIM), F32)],
        compiler_params=_cparams("parallel", "parallel"),
        name="sb_sample",
    )(q3, ck, cv, kv16, kv16)


def _fox_sample(q3, kv16, ck, cv, dcum, layer, nb, ts):
    p = ck.shape[2]
    q_spec, kp_spec, kn_spec, vn_spec, o_spec = _sample_specs(2, 4, 5, layer, ts, p)
    d0 = dcum[:, 0, p]
    return pl.pallas_call(
        _fox_s_kernel,
        grid=(nb, N_HEADS),
        in_specs=[
            pl.BlockSpec(memory_space=pltpu.SMEM),
            q_spec, kp_spec, kp_spec, kn_spec, vn_spec,
            pl.BlockSpec((1, 1, dcum.shape[2]), lambda b, h: (b * N_HEADS + h, 0, 0)),
        ],
        out_specs=o_spec,
        out_shape=jax.ShapeDtypeStruct((nb * ts, W_MIX), BF16),
        compiler_params=_cparams("parallel", "parallel"),
        name="fox_sample",
    )(d0, q3, ck, cv, kv16, kv16, dcum)


def _diff_sample(q3, kv16, ck, cv, bias_p, bias_n, lam_rows, gn, lam_init, layer, nb, ts):
    p = ck.shape[2]
    q_spec, kp_spec, kn_spec, vn_spec, o_spec = _sample_specs(1, 2, 3, layer, ts, p)
    stat = [pltpu.VMEM((ts, 1), F32), pltpu.VMEM((ts, HEAD_DIM), F32)]
    return pl.pallas_call(
        functools.partial(_diff_s_kernel, lam_init=lam_init),
        grid=(nb, N_HEADS),
        in_specs=[
            q_spec, kp_spec, kp_spec, kn_spec, vn_spec,
            pl.BlockSpec((1, ts, p), lambda b, h: (h, 0, 0)),
            pl.BlockSpec((1, ts, LANES), lambda b, h: (h, 0, 0)),
            pl.BlockSpec((8, LANES), lambda b, h: (0, 0)),
            pl.BlockSpec((1, HEAD_DIM), lambda b, h: (0, 0)),
        ],
        out_specs=o_spec,
        out_shape=jax.ShapeDtypeStruct((nb * ts, W_MIX), BF16),
        scratch_shapes=stat + stat,
        compiler_params=_cparams("parallel", "parallel"),
        name="diff_sample",
    )(q3, ck, cv, kv16, kv16, bias_p, bias_n, lam_rows, gn)


def _merge_kernel(oa_ref, ob_ref, oc_ref, ga_ref, gb_ref, gc_ref, wb_ref, o_ref):
    acc = ga_ref[...].astype(F32) * _dot(oa_ref[...], wb_ref[0])
    acc = acc + gb_ref[...].astype(F32) * _dot(ob_ref[...], wb_ref[1])
    acc = acc + gc_ref[...].astype(F32) * _dot(oc_ref[...], wb_ref[2])
    o_ref[...] = acc.astype(BF16)


def _merge(oa, ob, oc, gates, wb_all, layer, tm, tn):
    n = oa.shape[0]
    d = wb_all.shape[3]
    nj = d // tn
    o_spec = pl.BlockSpec((tm, W_MIX), lambda i, j: (i, 0))
    return pl.pallas_call(
        _merge_kernel,
        grid=(n // tm, nj),
        in_specs=[
            o_spec, o_spec, o_spec,
            pl.BlockSpec((tm, tn), lambda i, j: (i, j)),
            pl.BlockSpec((tm, tn), lambda i, j: (i, nj + j)),
            pl.BlockSpec((tm, tn), lambda i, j: (i, 2 * nj + j)),
            pl.BlockSpec((None, N_BRANCH, W_MIX, tn), lambda i, j: (layer, 0, 0, j)),
        ],
        out_specs=pl.BlockSpec((tm, tn), lambda i, j: (i, j)),
        out_shape=jax.ShapeDtypeStruct((n, d), BF16),
        compiler_params=_cparams("parallel", "parallel"),
        name="merge",
    )(oa, ob, oc, gates, gates, gates, wb_all)


def _resmm_kernel(a_ref, w_ref, x_ref, g_ref, o_ref):
    o_ref[...] = x_ref[...] + g_ref[0, 0] * _dot(a_ref[...], w_ref[0])


def _resmm(a, w_all, layer, x, mod, comp, tm, tn, tiles_per_batch):
    n, k = a.shape
    d = w_all.shape[2]
    return pl.pallas_call(
        _resmm_kernel,
        grid=(n // tm, d // tn),
        in_specs=[
            pl.BlockSpec((tm, k), lambda i, j: (i, 0)),
            pl.BlockSpec((1, k, tn), lambda i, j: (layer, 0, j)),
            pl.BlockSpec((tm, tn), lambda i, j: (i, j)),
            _mod_spec(mod, comp, tm, tn, tiles_per_batch, col_of=lambda j: j),
        ],
        out_specs=pl.BlockSpec((tm, tn), lambda i, j: (i, j)),
        out_shape=jax.ShapeDtypeStruct((n, d), F32),
        input_output_aliases={2: 0},
        compiler_params=_cparams("parallel", "parallel"),
        name="resmm",
    )(a, w_all, x, mod)


def _ffn_up_kernel(*refs, rows, experts_per_tile, expert_width):
    if experts_per_tile:
        x_ref, g_ref, sc_ref, sh_ref, wg_ref, wu_ref, cmb_ref, a_ref, h_ref = refs
    else:
        x_ref, g_ref, sc_ref, sh_ref, wg_ref, wu_ref, a_ref, h_ref = refs
    tm = x_ref.shape[0]

    @pl.when(pl.program_id(1) == 0)
    def _prologue():
        def chunk(r, carry):
            sl = pl.ds(pl.multiple_of(r * rows, rows), rows)
            h_ref[sl, :] = _norm_mod(x_ref[sl, :], g_ref[...], _mod_rows(sc_ref, sl), _mod_rows(sh_ref, sl)).astype(BF16)
            return carry
        lax.fori_loop(0, tm // rows, chunk, 0)

    h = h_ref[...]
    gt = _dot(h, wg_ref[0])
    a = gt * jax.nn.sigmoid(gt) * _dot(h, wu_ref[0])
    if not experts_per_tile:
        a_ref[...] = a.astype(BF16)
    else:
        for e in range(experts_per_tile):
            c = cmb_ref[e]
            for u in range(expert_width // LANES):
                off = e * expert_width + u * LANES
                a_ref[:, off:off + LANES] = (a[:, off:off + LANES] * c).astype(BF16)


def _ffn_up(x, mod, norm_g, wg_all, wu_all, layer, tm, tn, tiles_per_batch, combine=None, expert_width=0):
    n, d = x.shape
    f = wg_all.shape[2]
    ept = tn // expert_width if combine is not None else 0
    rows = min(tm, 256)
    in_specs = [
        pl.BlockSpec((tm, d), lambda i, j: (i, 0)),
        pl.BlockSpec((1, d), lambda i, j: (0, 0)),
        _mod_spec(mod, 4, tm, d, tiles_per_batch),
        _mod_spec(mod, 3, tm, d, tiles_per_batch),
        pl.BlockSpec((1, d, tn), lambda i, j: (layer, 0, j)),
        pl.BlockSpec((1, d, tn), lambda i, j: (layer, 0, j)),
    ]
    args = [x, norm_g, mod, mod, wg_all, wu_all]
    if combine is not None:
        in_specs.append(pl.BlockSpec((ept, tm, LANES), lambda i, j: (j, i, 0)))
        args.append(combine)
    return pl.pallas_call(
        functools.partial(_ffn_up_kernel, rows=rows, experts_per_tile=ept, expert_width=expert_width),
        grid=(n // tm, f // tn),
        in_specs=in_specs,
        out_specs=pl.BlockSpec((tm, tn), lambda i, j: (i, j)),
        out_shape=jax.ShapeDtypeStruct((n, f), BF16),
        scratch_shapes=[pltpu.VMEM((tm, d), BF16)],
        compiler_params=_cparams("parallel", "arbitrary"),
        name="ffn_up",
    )(*args)


def _router_kernel(x_ref, g_ref, sc_ref, sh_ref, wh_ref, wl_ref, b_ref, o_ref, *, rows, n_exp):
    tm = x_ref.shape[0]

    def chunk(r, carry):
        sl = pl.ds(pl.multiple_of(r * rows, rows), rows)
        h = _norm_mod(x_ref[sl, :], g_ref[...], _mod_rows(sc_ref, sl), _mod_rows(sh_ref, sl))
        hh = h.astype(BF16)
        hl = (h - hh.astype(F32)).astype(BF16)
        logits = _dot(hh, wh_ref[...]) + _dot(hl, wh_ref[...]) + _dot(hh, wl_ref[...]) + b_ref[...]
        lane = lax.broadcasted_iota(jnp.int32, logits.shape, 1).astype(F32)
        lg = jnp.where(lane < n_exp, logits, NEG)
        m1 = jnp.max(lg, axis=-1, keepdims=True)
        i1 = jnp.min(jnp.where(lg == m1, lane, float(LANES)), axis=-1, keepdims=True)
        lg2 = jnp.where(lane == i1, NEG, lg)
        m2 = jnp.max(lg2, axis=-1, keepdims=True)
        i2 = jnp.min(jnp.where(lg2 == m2, lane, float(LANES)), axis=-1, keepdims=True)
        e2 = jnp.exp(m2 - m1)
        p1 = 1.0 / (1.0 + e2)
        p2 = e2 / (1.0 + e2)
        for e in range(n_exp):
            ce = jnp.where(i1 == e, p1, 0.0) + jnp.where(i2 == e, p2, 0.0)
            o_ref[e, sl, :] = jnp.broadcast_to(ce, (rows, LANES))
        return carry
    lax.fori_loop(0, tm // rows, chunk, 0)


def _router(x, mod, norm_g, w_hi, w_lo, b_r, n_exp, tm, tiles_per_batch):
    n, d = x.shape
    rows = min(tm, 256)
    return pl.pallas_call(
        functools.partial(_router_kernel, rows=rows, n_exp=n_exp),
        grid=(n // tm,),
        in_specs=[
            pl.BlockSpec((tm, d), lambda i: (i, 0)),
            pl.BlockSpec((1, d), lambda i: (0, 0)),
            _mod_spec(mod, 4, tm, d, tiles_per_batch, with_j=False),
            _mod_spec(mod, 3, tm, d, tiles_per_batch, with_j=False),
            pl.BlockSpec((d, LANES), lambda i: (0, 0)),
            pl.BlockSpec((d, LANES), lambda i: (0, 0)),
            pl.BlockSpec((1, LANES), lambda i: (0, 0)),
        ],
        out_specs=pl.BlockSpec((n_exp, tm, LANES), lambda i: (0, i, 0)),
        out_shape=jax.ShapeDtypeStruct((n_exp, n, LANES), F32),
        compiler_params=_cparams("parallel"),
        name="router",
    )(x, norm_g, mod, mod, w_hi, w_lo, b_r)


def _final_norm_kernel(x_ref, g_ref, o_ref):
    x = x_ref[...]
    var = jnp.mean(x * x, axis=-1, keepdims=True)
    o_ref[...] = x * lax.rsqrt(var + EPS) * g_ref[...]


def _final_norm(x, g, tm):
    n, d = x.shape
    return pl.pallas_call(
        _final_norm_kernel,
        grid=(n // tm,),
        in_specs=[pl.BlockSpec((tm, d), lambda i: (i, 0)), pl.BlockSpec((1, d), lambda i: (0, 0))],
        out_specs=pl.BlockSpec((tm, d), lambda i: (i, 0)),
        out_shape=jax.ShapeDtypeStruct((n, d), F32),
        compiler_params=_cparams("parallel"),
        name="final_norm",
    )(x, g)


def _t5_bucket(rel):
    nb = N_BUCKETS // 2
    max_exact = nb // 2
    base = jnp.where(rel > 0, nb, 0)
    n = jnp.abs(rel)
    nf = jnp.maximum(n, 1).astype(F32)
    large = max_exact + (jnp.log(nf / max_exact) / math.log(MAX_DISTANCE / max_exact)
                         * (nb - max_exact)).astype(jnp.int32)
    large = jnp.minimum(large, nb - 1)
    return base + jnp.where(n < max_exact, n, large)


def _bias_table(rel_bias, qpos, kpos):
    bias = jnp.take(rel_bias, _t5_bucket(kpos[None, :] - qpos[:, None]), axis=0)
    vis = (kpos // CHUNK)[None, :] <= (qpos // CHUNK)[:, None]
    return jnp.where(vis[None], jnp.transpose(bias, (2, 0, 1)).astype(F32), NEG)


def _tiles(n_rows, t):
    tm = 512 if n_rows % 512 == 0 else n_rows
    tq = 512 if t % 512 == 0 and t >= 2048 else 128
    return tm, tq


def _trunk(x, mod, prm, past, nb, t):
    n, d = x.shape
    depth = prm["w_main"].shape[0]
    is_prompt = past is None
    tm, tq = _tiles(n, t)
    tm = min(tm, t) if is_prompt else n
    tiles_per_batch = t // tm if is_prompt else 1
    new_rows = []
    for l in range(depth):
        lam_init = 0.8 - 0.6 * math.exp(-0.3 * l)
        q3, kv32, kv16, gates, lfp = _inproj(x, mod[l], prm["norm1_g"][l], prm["w_main"], prm["w_f"][l], prm["b_f"][l],
                                             l, tm, tiles_per_batch)
        lf = lfp[:, :N_HEADS].reshape(nb, t, N_HEADS)
        new_rows.append((kv32, lf))
        lf_t = jnp.transpose(lf, (0, 2, 1))
        if is_prompt:
            dcum = _cumsum_rows(lf_t.reshape(nb * N_HEADS, t // LANES, LANES)).reshape(nb * N_HEADS, t)
            oa = _sb_prompt(q3, kv16, nb, t, tq, tq)
            ob = _diff_prompt(q3, kv16, prm["cfar"], prm["bsub"], prm["bdiag"], prm["lam_rows"][l], prm["diff_g"][l],
                              lam_init, nb, t, tq)
            oc = _fox_prompt(q3, kv16, dcum, nb, t, tq, tq)
        else:
            ck_a, cv_a, ck_b, cv_b, ck_c, cv_c, clf = past
            p = ck_a.shape[2]
            keys_pad = -(-(p + LANES) // (LANES * LANES)) * (LANES * LANES)
            lf_all = jnp.concatenate([jnp.transpose(clf[l], (0, 2, 1)), lf_t,
                                      jnp.zeros((nb, N_HEADS, keys_pad - p - t), F32)], axis=2)
            dcum = _cumsum_rows(lf_all.reshape(nb * N_HEADS, keys_pad // LANES, LANES)).reshape(nb * N_HEADS, 1, keys_pad)
            oa = _sb_sample(q3, kv16, ck_a, cv_a, l, nb, t, min(512, p))
            ob = _diff_sample(q3, kv16, ck_b, cv_b, prm["bias_sp"], prm["bias_sn"], prm["lam_rows"][l], prm["diff_g"][l],
                              lam_init, l, nb, t)
            oc = _fox_sample(q3, kv16, ck_c, cv_c, dcum, l, nb, t)
        merged = _merge(oa, ob, oc, gates, prm["w_branch"], l, tm, 512 if d % 512 == 0 else d)
        x = _resmm(merged, prm["w_out"], l, x, mod[l], 2, tm, 512 if d % 512 == 0 else d, tiles_per_batch)
        m = l // 2
        if l % 2 == 0:
            f_dim = prm["w_ffn_gate"].shape[2]
            tn = 256 if f_dim % 256 == 0 else LANES
            a = _ffn_up(x, mod[l], prm["norm2_g"][l], prm["w_ffn_gate"], prm["w_ffn_up"], m, tm, tn, tiles_per_batch)
            x = _resmm(a, prm["w_ffn_down"], m, x, mod[l], 5, tm, 512 if d % 512 == 0 else d, tiles_per_batch)
        else:
            n_exp, ew = prm["n_exp"], prm["expert_width"]
            combine = _router(x, mod[l], prm["norm2_g"][l], prm["w_router_hi"][m], prm["w_router_lo"][m],
                              prm["b_router"][m], n_exp, tm, tiles_per_batch)
            a = _ffn_up(x, mod[l], prm["norm2_g"][l], prm["w_exp_gate"], prm["w_exp_up"], m, tm, 2 * ew, tiles_per_batch,
                        combine=combine, expert_width=ew)
            x = _resmm(a, prm["w_exp_down"], m, x, mod[l], 5, tm, 512 if d % 512 == 0 else d, tiles_per_batch)
    y = _final_norm(x, prm["final_g"], min(tm, 256))
    outs = [jnp.stack([r[0][i] for r in new_rows]).reshape(depth, nb, t, N_HEADS, HEAD_DIM) for i in range(6)]
    outs.append(jnp.stack([r[1] for r in new_rows]))
    return y.reshape(nb, t, d), outs


def kernel(x_prompt, x_sample, c_prompt, c_sample, cache_k_a, cache_v_a, cache_k_b, cache_v_b, cache_k_c, cache_v_c,
           cache_lf_c, w_ada, b_ada, norm1_g, norm2_g, w_in, b_forget, lambda_q1, lambda_k1, lambda_q2, lambda_k2,
           diff_norm_g, rel_bias, w_branch, w_out, w_ffn_gate, w_ffn_up, w_ffn_down, w_router, b_router,
           w_exp_gate, w_exp_up, w_exp_down, final_norm_g):
    bp, tp, d = x_prompt.shape
    bs, ts, _ = x_sample.shape
    depth = w_in.shape[0]
    past_len = cache_k_a.shape[2]
    n_moe, n_exp, _, fe = w_exp_gate.shape
    ew = -(-fe // LANES) * LANES

    qkv = w_in[:, :, :9 * W_MIX].reshape(depth, d, 3, 3, W_MIX)
    w_main = jnp.concatenate([qkv[:, :, :, 0].reshape(depth, d, 3 * W_MIX),
                              qkv[:, :, :, 1:].reshape(depth, d, 6 * W_MIX),
                              w_in[:, :, 9 * W_MIX + N_HEADS:]], axis=2).astype(BF16)
    w_f = jnp.pad(w_in[:, :, 9 * W_MIX:9 * W_MIX + N_HEADS], ((0, 0), (0, 0), (0, LANES - N_HEADS))).astype(BF16)
    b_f = jnp.pad(b_forget, ((0, 0), (0, LANES - N_HEADS))).reshape(depth, 1, LANES)
    lam_rows = jnp.pad(jnp.stack([lambda_q1, lambda_k1, lambda_q2, lambda_k2], axis=1),
                       ((0, 0), (0, 4), (0, LANES - DIFF_HALF)))
    pad_e = ((0, 0), (0, 0), (0, 0), (0, ew - fe))
    w_eg = jnp.transpose(jnp.pad(w_exp_gate, pad_e), (0, 2, 1, 3)).reshape(n_moe, d, n_exp * ew).astype(BF16)
    w_eu = jnp.transpose(jnp.pad(w_exp_up, pad_e), (0, 2, 1, 3)).reshape(n_moe, d, n_exp * ew).astype(BF16)
    w_ed = jnp.pad(w_exp_down, ((0, 0), (0, 0), (0, ew - fe), (0, 0))).reshape(n_moe, n_exp * ew, d).astype(BF16)
    w_r = jnp.pad(w_router, ((0, 0), (0, 0), (0, LANES - n_exp)))
    w_r_hi = w_r.astype(BF16)
    w_r_lo = (w_r - w_r_hi.astype(F32)).astype(BF16)
    _, tq = _tiles(bp * tp, tp)
    pos = jnp.arange(tq, dtype=jnp.int32)
    qpos_s = past_len + jnp.arange(ts, dtype=jnp.int32)
    kpos_s = jnp.arange(past_len + LANES, dtype=jnp.int32)
    bias_s = jnp.where(kpos_s < past_len + ts, _bias_table(rel_bias, qpos_s, kpos_s), NEG)
    prm = dict(
        w_main=w_main, w_f=w_f, b_f=b_f, lam_rows=lam_rows,
        norm1_g=norm1_g.reshape(depth, 1, d), norm2_g=norm2_g.reshape(depth, 1, d),
        diff_g=diff_norm_g.reshape(depth, 1, HEAD_DIM), final_g=final_norm_g.reshape(1, d),
        w_branch=w_branch.astype(BF16), w_out=w_out.astype(BF16),
        w_ffn_gate=w_ffn_gate.astype(BF16), w_ffn_up=w_ffn_up.astype(BF16), w_ffn_down=w_ffn_down.astype(BF16),
        w_exp_gate=w_eg, w_exp_up=w_eu, w_exp_down=w_ed, n_exp=n_exp, expert_width=ew,
        w_router_hi=w_r_hi, w_router_lo=w_r_lo,
        b_router=jnp.pad(b_router, ((0, 0), (0, LANES - n_exp))).reshape(n_moe, 1, LANES),
        cfar=jnp.take(rel_bias, _t5_bucket(jnp.int32(-MAX_DISTANCE)), axis=0),
        bsub=_bias_table(rel_bias, tq + pos, pos), bdiag=_bias_table(rel_bias, pos, pos),
        bias_sp=bias_s[:, :, :past_len], bias_sn=bias_s[:, :, past_len:],
    )

    rows = -(-(bp + bs) // 16) * 16
    c_all = jnp.concatenate([c_prompt, c_sample, jnp.zeros((rows - bp - bs, d), F32)], axis=0)
    mod = _ada(c_all, w_ada, b_ada).reshape(depth, rows, 6, 1, d)
    mod_p = mod[:, :bp]
    mod_s = jnp.transpose(jnp.repeat(mod[:, bp:bp + bs, :, 0], ts, axis=1), (0, 2, 1, 3))[:, None]

    y_p, st_p = _trunk(x_prompt.reshape(bp * tp, d), mod_p, prm, None, bp, tp)
    caches = tuple(c.reshape(depth, bs, past_len, W_MIX) for c in
                   (cache_k_a, cache_v_a, cache_k_b, cache_v_b, cache_k_c, cache_v_c)) + (cache_lf_c,)
    y_s, st_s = _trunk(x_sample.reshape(bs * ts, d), mod_s, prm, caches, bs, ts)
    return (y_p, y_s, *st_p, *st_s)
```
